```python
import jax, jax.numpy as jnp
from jax import lax
import numpy as np

D_MODEL = 2048
BATCH = 4
SEQ = 4096
DEPTH = 2

FOX_HEADS = 8
FOX_HD = 128
FOX_WIDTH = FOX_HEADS * FOX_HD
Q_BLOCK = 128
MLSTM_HEADS = 4
MLSTM_DQK = 128
MLSTM_DV = 256
MLSTM_QK_WIDTH = MLSTM_HEADS * MLSTM_DQK
MLSTM_V_WIDTH = MLSTM_HEADS * MLSTM_DV
MLSTM_CHUNK = 64
D_FF = 5632
CONV_W = 3
EPS = 1e-6

OFF_FQ = 0
OFF_FK = OFF_FQ + FOX_WIDTH
OFF_FV = OFF_FK + FOX_WIDTH
OFF_FF = OFF_FV + FOX_WIDTH
OFF_MQ = OFF_FF + FOX_HEADS
OFF_MK = OFF_MQ + MLSTM_QK_WIDTH
OFF_MV = OFF_MK + MLSTM_QK_WIDTH
OFF_MO = OFF_MV + MLSTM_V_WIDTH
OFF_MI = OFF_MO + MLSTM_V_WIDTH
OFF_MF = OFF_MI + MLSTM_HEADS
OFF_GA = OFF_MF + MLSTM_HEADS
OFF_GM = OFF_GA + D_MODEL
D_IN = OFF_GM + D_MODEL

kernel_name = "hybrid_fox_mlstm_convffn"


def rmsnorm(x, g):
    x32 = x.astype(jnp.float32)
    y = x32 * lax.rsqrt(jnp.mean(x32 * x32, axis=-1, keepdims=True) + EPS)
    return (y * g.astype(jnp.float32)).astype(x.dtype)


def forgetting_attention(q, k, v, log_f):
    B, S, H, D = q.shape
    F = jnp.cumsum(log_f, axis=1).transpose(0, 2, 1)
    scale = D ** -0.5
    outs = []
    for blk in range(S // Q_BLOCK):
        q0, q1 = blk * Q_BLOCK, (blk + 1) * Q_BLOCK
        qb = q[:, q0:q1].astype(jnp.float32)
        kb = k[:, :q1].astype(jnp.float32)
        vb = v[:, :q1]
        logits = jnp.einsum('bqhd,bkhd->bhqk', qb, kb) * scale
        logits = logits + F[:, :, q0:q1, None] - F[:, :, None, :q1]
        q_pos = q0 + jnp.arange(Q_BLOCK)
        k_pos = jnp.arange(q1)
        causal = k_pos[None, :] <= q_pos[:, None]
        logits = jnp.where(causal, logits, -jnp.inf)
        p = jax.nn.softmax(logits, axis=-1)
        outs.append(jnp.einsum('bhqk,bkhd->bqhd', p.astype(vb.dtype), vb))
    return jnp.concatenate(outs, axis=1)


def mlstm_chunkwise(q, k, v, i_pre, log_f):
    B, S, H, DK = q.shape
    DV = v.shape[-1]
    L = MLSTM_CHUNK
    NC = S // L
    f32 = jnp.float32

    def chunks(a):
        return a.astype(f32).reshape(B, NC, L, H, a.shape[-1]).transpose(1, 0, 3, 2, 4)

    def gchunks(a):
        return a.astype(f32).reshape(B, NC, L, H).transpose(1, 0, 3, 2)

    qc = chunks(q) * (DK ** -0.5)
    kc, vc = chunks(k), chunks(v)
    ic, fc = gchunks(i_pre), gchunks(log_f)
    causal = jnp.tril(jnp.ones((L, L), dtype=bool))

    def step(carry, inp):
        C, n, m = carry
        qt, kt, vt, it, ft = inp
        b = jnp.cumsum(ft, axis=-1)
        dlog = b[..., :, None] - b[..., None, :] + it[..., None, :]
        dlog = jnp.where(causal, dlog, -jnp.inf)
        inter = b + m[..., None]
        m_row = jnp.maximum(inter, dlog.max(axis=-1))
        s = jnp.einsum('bhtd,bhsd->bhts', qt, kt) * jnp.exp(dlog - m_row[..., None])
        inter_w = jnp.exp(inter - m_row)
        num = inter_w[..., None] * jnp.einsum('bhtd,bhde->bhte', qt, C) + jnp.einsum('bhts,bhse->bhte', s, vt)
        den = inter_w * jnp.einsum('bhtd,bhd->bht', qt, n) + s.sum(axis=-1)
        h = num / jnp.maximum(jnp.abs(den), jnp.exp(-m_row))[..., None]
        bL = b[..., -1]
        w_log = bL[..., None] - b + it
        m_new = jnp.maximum(bL + m, w_log.max(axis=-1))
        decay = jnp.exp(bL + m - m_new)
        ws = jnp.exp(w_log - m_new[..., None])
        C_new = decay[..., None, None] * C + jnp.einsum('bhs,bhsd,bhse->bhde', ws, kt, vt)
        n_new = decay[..., None] * n + jnp.einsum('bhs,bhsd->bhd', ws, kt)
        return (C_new, n_new, m_new), h

    init = (jnp.zeros((B, H, DK, DV), f32), jnp.zeros((B, H, DK), f32), jnp.zeros((B, H), f32))
    _, hs = lax.scan(step, init, (qc, kc, vc, ic, fc))
    return hs.transpose(1, 0, 3, 2, 4).reshape(B, S, H, DV).astype(v.dtype)


def causal_dwconv(u, w, b):
    S = u.shape[1]
    up = jnp.pad(u, ((0, 0), (CONV_W - 1, 0), (0, 0)))
    out = b
    for j in range(CONV_W):
        out = out + up[:, j:j + S] * w[j]
    return out


def setup_inputs(seed: int = 0) -> dict:
    key = jax.random.key(seed)
    ks = jax.random.split(key, 20)
    nrm = lambda k, shape, s: jax.random.normal(k, shape, jnp.float32) * s
    L = DEPTH
    return {
        "x": nrm(ks[0], (BATCH, SEQ, D_MODEL), 1.0),
        "norm1_g": 1.0 + nrm(ks[1], (L, D_MODEL), 0.1),
        "w_in": nrm(ks[2], (L, D_MODEL, D_IN), D_MODEL ** -0.5),
        "fox_f_bias": jnp.linspace(1.0, 4.0, FOX_HEADS, dtype=jnp.float32)[None] + nrm(ks[3], (L, FOX_HEADS), 0.1),
        "q_norm_g": 1.0 + nrm(ks[4], (L, FOX_HD), 0.1),
        "k_norm_g": 1.0 + nrm(ks[5], (L, FOX_HD), 0.1),
        "m_i_bias": nrm(ks[6], (L, MLSTM_HEADS), 0.1),
        "m_f_bias": jnp.linspace(3.0, 6.0, MLSTM_HEADS, dtype=jnp.float32)[None] + nrm(ks[7], (L, MLSTM_HEADS), 0.1),
        "m_norm_g": 1.0 + nrm(ks[8], (L, MLSTM_HEADS, MLSTM_DV), 0.1),
        "w_proj_a": nrm(ks[9], (L, FOX_WIDTH, D_MODEL), FOX_WIDTH ** -0.5),
        "w_proj_m": nrm(ks[10], (L, MLSTM_V_WIDTH, D_MODEL), MLSTM_V_WIDTH ** -0.5),
        "w_out": nrm(ks[11], (L, D_MODEL, D_MODEL), D_MODEL ** -0.5),
        "norm2_g": 1.0 + nrm(ks[12], (L, D_MODEL), 0.1),
        "w_up": nrm(ks[13], (L, D_MODEL, 2 * D_FF), D_MODEL ** -0.5),
        "conv_w": nrm(ks[14], (L, CONV_W, 2 * D_FF), CONV_W ** -0.5),
        "conv_b": nrm(ks[15], (L, 2 * D_FF), 0.02),
        "w_down": nrm(ks[16], (L, D_FF, D_MODEL), D_FF ** -0.5),
    }


def reference(x, norm1_g, w_in, fox_f_bias, q_norm_g, k_norm_g, m_i_bias, m_f_bias,
              m_norm_g, w_proj_a, w_proj_m, w_out, norm2_g, w_up, conv_w, conv_b, w_down):
    B, S, _ = x.shape

    for l in range(DEPTH):
        h = rmsnorm(x, norm1_g[l])
        z = jnp.einsum('bsd,de->bse', h, w_in[l])

        fq = rmsnorm(z[..., OFF_FQ:OFF_FK].reshape(B, S, FOX_HEADS, FOX_HD), q_norm_g[l])
        fk = rmsnorm(z[..., OFF_FK:OFF_FV].reshape(B, S, FOX_HEADS, FOX_HD), k_norm_g[l])
        fv = z[..., OFF_FV:OFF_FF].reshape(B, S, FOX_HEADS, FOX_HD)
        f_logf = jax.nn.log_sigmoid((z[..., OFF_FF:OFF_MQ] + fox_f_bias[l]).astype(jnp.float32))
        a_out = forgetting_attention(fq, fk, fv, f_logf).reshape(B, S, FOX_WIDTH)

        mq = z[..., OFF_MQ:OFF_MK].reshape(B, S, MLSTM_HEADS, MLSTM_DQK)
        mk = z[..., OFF_MK:OFF_MV].reshape(B, S, MLSTM_HEADS, MLSTM_DQK)
        mv = z[..., OFF_MV:OFF_MO].reshape(B, S, MLSTM_HEADS, MLSTM_DV)
        m_o = jax.nn.sigmoid(z[..., OFF_MO:OFF_MI]).reshape(B, S, MLSTM_HEADS, MLSTM_DV)
        m_ipre = (z[..., OFF_MI:OFF_MF] + m_i_bias[l]).astype(jnp.float32)
        m_logf = jax.nn.log_sigmoid((z[..., OFF_MF:OFF_GA] + m_f_bias[l]).astype(jnp.float32))
        hm = mlstm_chunkwise(mq, mk, mv, m_ipre, m_logf)
        m_out = (rmsnorm(hm, m_norm_g[l]) * m_o).reshape(B, S, MLSTM_V_WIDTH)

        g_a = jax.nn.sigmoid(z[..., OFF_GA:OFF_GM])
        g_m = jax.nn.sigmoid(z[..., OFF_GM:D_IN])
        merged = (g_a * jnp.einsum('bse,ed->bsd', a_out, w_proj_a[l])
                  + g_m * jnp.einsum('bse,ed->bsd', m_out, w_proj_m[l]))
        x = x + jnp.einsum('bsd,de->bse', merged, w_out[l])

        h2 = rmsnorm(x, norm2_g[l])
        u = causal_dwconv(jnp.einsum('bsd,df->bsf', h2, w_up[l]), conv_w[l], conv_b[l])
        u_gate, u_val = u[..., :D_FF], u[..., D_FF:]
        x = x + jnp.einsum('bsf,fd->bsd', jax.nn.silu(u_gate) * u_val, w_down[l])
    return x
```

```python
import functools

import jax
import jax.numpy as jnp
from jax import lax
from jax.experimental import pallas as pl
from jax.experimental.pallas import tpu as pltpu

F32 = jnp.float32
BF16 = jnp.bfloat16

D_MODEL = 2048
FOX_HEADS = 8
FOX_HD = 128
FOX_WIDTH = FOX_HEADS * FOX_HD
M_HEADS = 4
M_DQK = 128
M_DV = 256
M_QK_WIDTH = M_HEADS * M_DQK
M_V_WIDTH = M_HEADS * M_DV
D_FF = 5632
EPS = 1e-6
N_GATES = 16

OFF_FF = 3 * FOX_WIDTH
OFF_MQ = OFF_FF + FOX_HEADS
OFF_MI = OFF_MQ + 2 * M_QK_WIDTH + 2 * M_V_WIDTH
OFF_GA = OFF_MI + 2 * M_HEADS
D_IN = OFF_GA + 2 * D_MODEL

ZC_FQ = 0
ZC_FK = ZC_FQ + FOX_WIDTH
ZC_FV = ZC_FK + FOX_WIDTH
ZC_MQ = ZC_FV + FOX_WIDTH
ZC_MK = ZC_MQ + M_QK_WIDTH
ZC_MV = ZC_MK + M_QK_WIDTH
ZC_MO = ZC_MV + M_V_WIDTH
ZC_GA = ZC_MO + M_V_WIDTH
ZC_GM = ZC_GA + D_MODEL
Z_WIDTH = ZC_GM + D_MODEL

GR_FOX = 0
GR_MI = FOX_HEADS
GR_MF = FOX_HEADS + M_HEADS

LANES = 128
VMEM_LIMIT = 56 * 1024 * 1024

INPROJ_TM = 1024
INPROJ_TN = 1024
FOX_T = 512
MLSTM_L = 64
MERGE_TM = 512
FFN_TM = 512
FFN_TN = 512
HALO = 16


def _params(sem):
    return pltpu.CompilerParams(dimension_semantics=sem, vmem_limit_bytes=VMEM_LIMIT)


def _rms(x, g):
    ms = jnp.mean(x * x, axis=-1, keepdims=True)
    return x * lax.rsqrt(ms + EPS) * g


def _dot(a, b):
    return jnp.dot(a, b, preferred_element_type=F32)


def _dot_nt(a, b):
    return lax.dot_general(a, b, (((1,), (1,)), ((), ())), preferred_element_type=F32)


def _dot_tn(a, b):
    return lax.dot_general(a, b, (((0,), (0,)), ((), ())), preferred_element_type=F32)


def _inproj_kernel(x_ref, g1_ref, w_ref, qkg_ref, wgt_ref, z_ref, zg_ref, h_scr, *, n_qk_tiles):
    j = pl.program_id(1)

    @pl.when(j == 0)
    def _():
        h = _rms(x_ref[...], g1_ref[...]).astype(BF16)
        h_scr[...] = h
        zg_ref[...] = _dot_nt(wgt_ref[...], h)

    acc = _dot(h_scr[...], w_ref[...])

    @pl.when(j < n_qk_tiles)
    def _():
        for c in range(acc.shape[1] // FOX_HD):
            sl = slice(c * FOX_HD, (c + 1) * FOX_HD)
            z_ref[:, sl] = _rms(acc[:, sl], qkg_ref[:, sl]).astype(BF16)

    @pl.when(j >= n_qk_tiles)
    def _():
        z_ref[...] = acc.astype(BF16)


def _inproj(x, g1, w_main, qk_gain, w_gates_t):
    t = x.shape[0]
    tm, tn = min(INPROJ_TM, t), INPROJ_TN
    n_qk_tiles = (2 * FOX_WIDTH) // tn
    return pl.pallas_call(
        functools.partial(_inproj_kernel, n_qk_tiles=n_qk_tiles),
        grid=(t // tm, Z_WIDTH // tn),
        in_specs=[
            pl.BlockSpec((tm, D_MODEL), lambda i, j: (i, 0)),
            pl.BlockSpec((1, D_MODEL), lambda i, j: (0, 0)),
            pl.BlockSpec((D_MODEL, tn), lambda i, j: (0, j)),
            pl.BlockSpec((1, tn), lambda i, j: (0, jnp.minimum(j, n_qk_tiles - 1))),
            pl.BlockSpec((N_GATES, D_MODEL), lambda i, j: (0, 0)),
        ],
        out_specs=[
            pl.BlockSpec((tm, tn), lambda i, j: (i, j)),
            pl.BlockSpec((N_GATES, tm), lambda i, j: (0, i)),
        ],
        out_shape=[
            jax.ShapeDtypeStruct((t, Z_WIDTH), BF16),
            jax.ShapeDtypeStruct((N_GATES, t), F32),
        ],
        scratch_shapes=[pltpu.VMEM((tm, D_MODEL), BF16)],
        compiler_params=_params(("arbitrary", "arbitrary")),
        name="inproj",
    )(x, g1, w_main, qk_gain, w_gates_t)


def _log_sigmoid(x):
    return jnp.minimum(x, 0.0) - jnp.log1p(jnp.exp(-jnp.abs(x)))


def _gates_kernel(zg_ref, bias_ref, ff_ref, gm_ref, gc_ref, *, seq, fox_t, chunk):
    z = zg_ref[...] + bias_ref[...]
    row = lax.broadcasted_iota(jnp.int32, z.shape, 0)
    lane = lax.broadcasted_iota(jnp.int32, z.shape, 1)
    pos = jnp.where(row < GR_MI, lane, lane % chunk)
    c = _log_sigmoid(z)
    k = 1
    while k < seq:
        c = c + jnp.where(pos >= k, pltpu.roll(c, k, axis=1), 0.0)
        k *= 2
    g = jnp.where((row >= GR_MI) & (row < GR_MF), z, c)

    for r in range(FOX_HEADS):
        for b in range(seq // fox_t):
            ff_ref[0, r, pl.ds(b, 1), :] = g[r:r + 1, b * fox_t:(b + 1) * fox_t]
    for b in range(seq // chunk):
        gm_ref[b] = g[GR_MI:, b * chunk:(b + 1) * chunk]
    pad = jnp.zeros((LANES - N_GATES, seq), F32)
    gc_ref[...] = jnp.transpose(jnp.concatenate([g, pad], axis=0))


def _gates(zg_t, bias, batch, seq):
    t = batch * seq
    fox_t, chunk = min(FOX_T, seq), MLSTM_L
    n_chunks = seq // chunk
    return pl.pallas_call(
        functools.partial(_gates_kernel, seq=seq, fox_t=fox_t, chunk=chunk),
        grid=(batch,),
        in_specs=[
            pl.BlockSpec((N_GATES, seq), lambda b: (0, b)),
            pl.BlockSpec((N_GATES, 1), lambda b: (0, 0)),
        ],
        out_specs=[
            pl.BlockSpec((1, FOX_HEADS, seq // fox_t, fox_t), lambda b: (b, 0, 0, 0)),
            pl.BlockSpec((n_chunks, 2 * M_HEADS, chunk), lambda b: (b, 0, 0)),
            pl.BlockSpec((seq, LANES), lambda b: (b, 0)),
        ],
        out_shape=[
            jax.ShapeDtypeStruct((batch, FOX_HEADS, seq // fox_t, fox_t), F32),
            jax.ShapeDtypeStruct((t // chunk, 2 * M_HEADS, chunk), F32),
            jax.ShapeDtypeStruct((t, LANES), F32),
        ],
        compiler_params=_params(("arbitrary",)),
        name="gates",
    )(zg_t, bias)


def _fox_kernel(q_ref, k_ref, v_ref, f_ref, o_ref, *, tile):
    qi = pl.program_id(2)
    q = q_ref[...]
    f_q_row = f_ref[0, 0, pl.ds(qi, 1), :]
    f_q = jnp.transpose(jnp.broadcast_to(f_q_row, (LANES, tile)))[:, 0:1]
    row = lax.broadcasted_iota(jnp.int32, (tile, tile), 0)
    col = lax.broadcasted_iota(jnp.int32, (tile, tile), 1)

    def step(c, carry, diagonal):
        m, l, acc = carry
        start = pl.multiple_of(c * tile, tile)
        k = k_ref[pl.ds(start, tile), :]
        v = v_ref[pl.ds(start, tile), :]
        s = (_dot_nt(q, k) + f_q) - f_ref[0, 0, pl.ds(c, 1), :]
        if diagonal:
            s = jnp.where(col <= row, s, -jnp.inf)
        m_new = jnp.maximum(m, jnp.max(s, axis=-1, keepdims=True))
        alpha = jnp.exp(m - m_new)
        p = jnp.exp(s - m_new)
        l = alpha * l + jnp.sum(p, axis=-1, keepdims=True)
        acc = alpha * acc + _dot(p.astype(BF16), v)
        return m_new, l, acc

    init = (jnp.full((tile, 1), -jnp.inf, F32), jnp.zeros((tile, 1), F32), jnp.zeros((tile, FOX_HD), F32))
    carry = lax.fori_loop(0, qi, lambda c, cr: step(c, cr, False), init)
    _, l, acc = step(qi, carry, True)
    o_ref[...] = (acc / l).astype(BF16)


def _fox(z, f_fox, batch, seq):
    t = batch * seq
    tile = min(FOX_T, seq)
    nq = seq // tile
    return pl.pallas_call(
        functools.partial(_fox_kernel, tile=tile),
        grid=(batch, FOX_HEADS, nq),
        in_specs=[
            pl.BlockSpec((tile, FOX_HD), lambda b, h, i: (b * nq + i, ZC_FQ // FOX_HD + h)),
            pl.BlockSpec((seq, FOX_HD), lambda b, h, i: (b, ZC_FK // FOX_HD + h)),
            pl.BlockSpec((seq, FOX_HD), lambda b, h, i: (b, ZC_FV // FOX_HD + h)),
            pl.BlockSpec((1, 1, nq, tile), lambda b, h, i: (b, h, 0, 0)),
        ],
        out_specs=pl.BlockSpec((tile, FOX_HD), lambda b, h, i: (b * nq + i, h)),
        out_shape=jax.ShapeDtypeStruct((t, FOX_WIDTH), BF16),
        compiler_params=_params(("arbitrary", "arbitrary", "arbitrary")),
        name="fox",
    )(z, z, z, f_fox)


def _mlstm_kernel(q_ref, k_ref, v_ref, o_ref, grow_ref, gcol_ref, ng_ref, out_ref, cx_scr, m_scr, *, chunk):
    @pl.when(pl.program_id(1) == 0)
    def _():
        cx_scr[...] = jnp.zeros_like(cx_scr)
        m_scr[...] = jnp.zeros_like(m_scr)

    t_idx = lax.broadcasted_iota(jnp.int32, (chunk, chunk), 0)
    s_idx = lax.broadcasted_iota(jnp.int32, (chunk, chunk), 1)
    causal = s_idx <= t_idx
    ones_col = (lax.broadcasted_iota(jnp.int32, (chunk, LANES), 1) == 0).astype(BF16)
    scale = M_DQK ** -0.5

    for h in range(M_HEADS):
        qk_sl = slice(h * M_DQK, (h + 1) * M_DQK)
        v_sl = slice(h * M_DV, (h + 1) * M_DV)
        q = (q_ref[:, qk_sl].astype(F32) * scale).astype(BF16)
        k = k_ref[:, qk_sl]
        vx = jnp.concatenate([v_ref[:, v_sl], ones_col], axis=1)
        i_row = grow_ref[0, h:h + 1, :]
        b_row = grow_ref[0, M_HEADS + h:M_HEADS + h + 1, :]
        i_col = gcol_ref[:, GR_MI + h:GR_MI + h + 1]
        b_col = gcol_ref[:, GR_MF + h:GR_MF + h + 1]
        m_prev = m_scr[h, 0:1, 0:1]
        cx = cx_scr[h]

        dlog = jnp.where(causal, (b_col - b_row) + i_row, -jnp.inf)
        inter = b_col + m_prev
        m_row = jnp.maximum(inter, jnp.max(dlog, axis=-1, keepdims=True))
        s = _dot_nt(q, k) * jnp.exp(dlog - m_row)
        inter_w = jnp.exp(inter - m_row)
        nd = inter_w * _dot(q, cx.astype(BF16)) + _dot(s.astype(BF16), vx)
        num = nd[:, :M_DV]
        den = nd[:, M_DV:M_DV + 1]
        hm = num / jnp.maximum(jnp.abs(den), jnp.exp(-m_row))
        gate = jax.nn.sigmoid(o_ref[:, v_sl].astype(F32))
        out_ref[:, v_sl] = (_rms(hm, ng_ref[:, v_sl]) * gate).astype(BF16)

        b_last = b_col[chunk - 1:chunk, :]
        w_log = (b_last - b_col) + i_col
        m_new = jnp.maximum(b_last + m_prev, jnp.max(w_log, axis=0, keepdims=True))
        decay = jnp.exp(b_last + m_prev - m_new)
        kw = (k.astype(F32) * jnp.exp(w_log - m_new)).astype(BF16)
        cx_scr[h] = decay * cx + _dot_tn(kw, vx)
        m_scr[h] = jnp.broadcast_to(m_new, m_scr.shape[1:])


def _mlstm(z, g_rows, g_cols, norm_g, batch, seq):
    t = batch * seq
    chunk = MLSTM_L
    nc = seq // chunk
    return pl.pallas_call(
        functools.partial(_mlstm_kernel, chunk=chunk),
        grid=(batch, nc),
        in_specs=[
            pl.BlockSpec((chunk, M_QK_WIDTH), lambda b, c: (b * nc + c, ZC_MQ // M_QK_WIDTH)),
            pl.BlockSpec((chunk, M_QK_WIDTH), lambda b, c: (b * nc + c, ZC_MK // M_QK_WIDTH)),
            pl.BlockSpec((chunk, M_V_WIDTH), lambda b, c: (b * nc + c, ZC_MV // M_V_WIDTH)),
            pl.BlockSpec((chunk, M_V_WIDTH), lambda b, c: (b * nc + c, ZC_MO // M_V_WIDTH)),
            pl.BlockSpec((1, 2 * M_HEADS, chunk), lambda b, c: (b * nc + c, 0, 0)),
            pl.BlockSpec((chunk, LANES), lambda b, c: (b * nc + c, 0)),
            pl.BlockSpec((1, M_V_WIDTH), lambda b, c: (0, 0)),
        ],
        out_specs=pl.BlockSpec((chunk, M_V_WIDTH), lambda b, c: (b * nc + c, 0)),
        out_shape=jax.ShapeDtypeStruct((t, M_V_WIDTH), BF16),
        scratch_shapes=[
            pltpu.VMEM((M_HEADS, M_DQK, M_DV + LANES), F32),
            pltpu.VMEM((M_HEADS, 8, LANES), F32),
        ],
        compiler_params=_params(("arbitrary", "arbitrary")),
        name="mlstm",
    )(z, z, z, z, g_rows, g_cols, norm_g)


def _merge_kernel(x_ref, a_ref, m_ref, ga_ref, gm_ref, pa_ref, pm_ref, wo_ref, out_ref):
    g_a = jax.nn.sigmoid(ga_ref[...].astype(F32))
    g_m = jax.nn.sigmoid(gm_ref[...].astype(F32))
    merged = g_a * _dot(a_ref[...], pa_ref[...]) + g_m * _dot(m_ref[...], pm_ref[...])
    out_ref[...] = x_ref[...] + _dot(merged.astype(BF16), wo_ref[...])


def _merge(x, a_out, m_out, z, w_pa, w_pm, w_o):
    t = x.shape[0]
    tm = min(MERGE_TM, t)
    resident = functools.partial(pl.BlockSpec, index_map=lambda i: (0, 0), pipeline_mode=pl.Buffered(1))
    return pl.pallas_call(
        _merge_kernel,
        grid=(t // tm,),
        in_specs=[
            pl.BlockSpec((tm, D_MODEL), lambda i: (i, 0)),
            pl.BlockSpec((tm, FOX_WIDTH), lambda i: (i, 0)),
            pl.BlockSpec((tm, M_V_WIDTH), lambda i: (i, 0)),
            pl.BlockSpec((tm, D_MODEL), lambda i: (i, ZC_GA // D_MODEL)),
            pl.BlockSpec((tm, D_MODEL), lambda i: (i, ZC_GM // D_MODEL)),
            resident((FOX_WIDTH, D_MODEL)),
            resident((M_V_WIDTH, D_MODEL)),
            resident((D_MODEL, D_MODEL)),
        ],
        out_specs=pl.BlockSpec((tm, D_MODEL), lambda i: (i, 0)),
        out_shape=jax.ShapeDtypeStruct((t, D_MODEL), F32),
        compiler_params=_params(("arbitrary",)),
        name="merge",
    )(x, a_out, m_out, z, z, w_pa, w_pm, w_o)


def _shift_rows(u, halo, n):
    rows = lax.broadcasted_iota(jnp.int32, u.shape, 0)
    out = pltpu.roll(u, n, axis=0)
    for r in range(n):
        out = jnp.where(rows == r, halo[HALO - n + r:HALO - n + r + 1, :], out)
    return out


def _causal_conv(u, halo, w_ref, b_ref):
    out = b_ref[...] + _shift_rows(u, halo, 2) * w_ref[0:1, :]
    out = out + _shift_rows(u, halo, 1) * w_ref[1:2, :]
    return out + u * w_ref[2:3, :]


def _ffn_kernel(x_ref, xh_ref, g2_ref, wg_ref, wv_ref, cwg_ref, cwv_ref, cbg_ref, cbv_ref, wd_ref, out_ref,
                h_scr, hh_scr, *, tiles_per_seq):
    i = pl.program_id(0)
    j = pl.program_id(1)

    @pl.when(j == 0)
    def _():
        x = x_ref[...]
        h_scr[...] = _rms(x, g2_ref[...]).astype(BF16)
        hh_scr[...] = _rms(xh_ref[...], g2_ref[...]).astype(BF16)
        out_ref[...] = x

    keep = (i % tiles_per_seq != 0).astype(F32)
    h = h_scr[...]
    hh = hh_scr[...]
    u_gate = _causal_conv(_dot(h, wg_ref[...]), _dot(hh, wg_ref[...]) * keep, cwg_ref, cbg_ref)
    u_val = _causal_conv(_dot(h, wv_ref[...]), _dot(hh, wv_ref[...]) * keep, cwv_ref, cbv_ref)
    act = (u_gate * jax.nn.sigmoid(u_gate) * u_val).astype(BF16)
    out_ref[...] += _dot(act, wd_ref[...])


def _ffn(x, g2, w_up, conv_w, conv_b, w_down, seq):
    t = x.shape[0]
    tm, tn = min(FFN_TM, seq), FFN_TN
    nj = D_FF // tn
    halo_blocks = tm // HALO
    return pl.pallas_call(
        functools.partial(_ffn_kernel, tiles_per_seq=seq // tm),
        grid=(t // tm, nj),
        in_specs=[
            pl.BlockSpec((tm, D_MODEL), lambda i, j: (i, 0)),
            pl.BlockSpec((HALO, D_MODEL), lambda i, j: (jnp.maximum(i * halo_blocks - 1, 0), 0)),
            pl.BlockSpec((1, D_MODEL), lambda i, j: (0, 0)),
            pl.BlockSpec((D_MODEL, tn), lambda i, j: (0, j)),
            pl.BlockSpec((D_MODEL, tn), lambda i, j: (0, nj + j)),
            pl.BlockSpec((3, tn), lambda i, j: (0, j)),
            pl.BlockSpec((3, tn), lambda i, j: (0, nj + j)),
            pl.BlockSpec((1, tn), lambda i, j: (0, j)),
            pl.BlockSpec((1, tn), lambda i, j: (0, nj + j)),
            pl.BlockSpec((tn, D_MODEL), lambda i, j: (j, 0)),
        ],
        out_specs=pl.BlockSpec((tm, D_MODEL), lambda i, j: (i, 0)),
        out_shape=jax.ShapeDtypeStruct((t, D_MODEL), F32),
        scratch_shapes=[pltpu.VMEM((tm, D_MODEL), BF16), pltpu.VMEM((HALO, D_MODEL), BF16)],
        compiler_params=_params(("arbitrary", "arbitrary")),
        name="ffn",
    )(x, x, g2, w_up, w_up, conv_w, conv_w, conv_b, conv_b, w_down)


def _row(v):
    return v.reshape(1, -1).astype(F32)


def kernel(x, norm1_g, w_in, fox_f_bias, q_norm_g, k_norm_g, m_i_bias, m_f_bias, m_norm_g, w_proj_a, w_proj_m,
           w_out, norm2_g, w_up, conv_w, conv_b, w_down):
    batch, seq, _ = x.shape
    depth = w_in.shape[0]
    assert seq % FOX_T == 0 and seq % FFN_TM == 0 and seq % (8 * MLSTM_L) == 0, seq
    assert (batch * seq) % INPROJ_TM == 0 and (batch * seq) % MERGE_TM == 0, (batch, seq)
    xf = x.reshape(batch * seq, D_MODEL).astype(F32)
    for l in range(depth):
        w = w_in[l]
        w_main = jnp.concatenate([w[:, :OFF_FF], w[:, OFF_MQ:OFF_MI], w[:, OFF_GA:]], axis=1).astype(BF16)
        w_gates_t = jnp.concatenate([w[:, OFF_FF:OFF_MQ], w[:, OFF_MI:OFF_GA]], axis=1).T.astype(BF16)
        gate_bias = jnp.concatenate([fox_f_bias[l], m_i_bias[l], m_f_bias[l]]).reshape(N_GATES, 1).astype(F32)
        qk_gain = _row(jnp.concatenate([jnp.tile(q_norm_g[l] * FOX_HD ** -0.5, FOX_HEADS),
                                        jnp.tile(k_norm_g[l], FOX_HEADS)]))

        z, zg_t = _inproj(xf, _row(norm1_g[l]), w_main, qk_gain, w_gates_t)
        f_fox, g_rows, g_cols = _gates(zg_t, gate_bias, batch, seq)
        a_out = _fox(z, f_fox, batch, seq)
        m_out = _mlstm(z, g_rows, g_cols, _row(m_norm_g[l]), batch, seq)
        xf = _merge(xf, a_out, m_out, z, w_proj_a[l].astype(BF16), w_proj_m[l].astype(BF16), w_out[l].astype(BF16))
        xf = _ffn(xf, _row(norm2_g[l]), w_up[l].astype(BF16), conv_w[l].astype(F32), _row(conv_b[l]),
                  w_down[l].astype(BF16), seq)
    return xf.reshape(batch, seq, D_MODEL).astype(x.dtype)
```
